```python
import jax, jax.numpy as jnp
from jax import lax
import numpy as np

D_MODEL = 4096
BATCH = 4
SEQ = 4096
DEPTH = 2
DEC_BATCH = 4
DEC_SEQ = 2048
PAST_LEN = 128

HEAD_DIM = 128
D_MIX = D_MODEL
N_HEADS = D_MIX // HEAD_DIM
N_GROUPS_FOURIER = N_HEADS // 4
N_HEADS_SC = (N_HEADS - N_GROUPS_FOURIER) // 2
N_HEADS_CF = N_HEADS - N_GROUPS_FOURIER - N_HEADS_SC
D_SC = N_HEADS_SC * HEAD_DIM
D_FOURIER = N_GROUPS_FOURIER * HEAD_DIM
D_CF = N_HEADS_CF * HEAD_DIM
D_IN = 3 * D_SC + D_FOURIER + 2 * D_CF
SC_KERNEL = 3
CF_KERNEL = 31
FFN_KERNEL = 3
D_FF = ((8 * D_MODEL // 3 + 255) // 256) * 256
EPS = 1e-6

kernel_name = "hybrid_shortconv_fourier_conformer_encoder"


def rmsnorm(x, g):
    xf = x.astype(jnp.float32)
    y = xf * lax.rsqrt(jnp.mean(xf * xf, axis=-1, keepdims=True) + EPS)
    return y.astype(x.dtype) * g


def layernorm(x, g, b):
    xf = x.astype(jnp.float32)
    mu = jnp.mean(xf, axis=-1, keepdims=True)
    xc = xf - mu
    y = xc * lax.rsqrt(jnp.mean(xc * xc, axis=-1, keepdims=True) + EPS)
    return y.astype(x.dtype) * g + b


def dwconv(x, w):
    c = w.shape[1]
    return lax.conv_general_dilated(
        x, w[:, None, :].astype(x.dtype), window_strides=(1,), padding="SAME",
        dimension_numbers=("NWC", "WIO", "NWC"), feature_group_count=c)


def fourier_mix(u):
    b, s, _ = u.shape
    uf = u.astype(jnp.float32).reshape(b, s, N_GROUPS_FOURIER, HEAD_DIM)
    y = jnp.fft.fft2(uf, axes=(1, 3), norm="ortho").real
    return y.reshape(b, s, D_FOURIER).astype(u.dtype)


def mixer(x, norm_g, w_in, sc_conv_w, cf_conv_w, cf_conv_b, cf_ln_g, cf_ln_b, w_out):
    h = rmsnorm(x, norm_g)
    p = h @ w_in
    o1 = D_SC
    o2 = 2 * D_SC
    o3 = 3 * D_SC
    o4 = o3 + D_FOURIER
    o5 = o4 + D_CF
    b_gate, c_gate, v = p[..., :o1], p[..., o1:o2], p[..., o2:o3]
    f = p[..., o3:o4]
    cu, cg = p[..., o4:o5], p[..., o5:]
    y_sc = b_gate * dwconv(c_gate * v, sc_conv_w)
    y_f = fourier_mix(f)
    g = cu * jax.nn.sigmoid(cg)
    g = dwconv(g, cf_conv_w) + cf_conv_b
    y_cf = jax.nn.silu(layernorm(g, cf_ln_g, cf_ln_b))
    y = jnp.concatenate([y_sc, y_f, y_cf], axis=-1)
    return y @ w_out


def conv_ffn(x, norm_g, w_up, ffn_conv_w, w_down):
    h = rmsnorm(x, norm_g)
    up = h @ w_up
    gate, val = up[..., :D_FF], up[..., D_FF:]
    return (jax.nn.silu(dwconv(gate, ffn_conv_w)) * val) @ w_down


def trunk(x, norm_mix_g, w_in, sc_conv_w, cf_conv_w, cf_conv_b, cf_ln_g, cf_ln_b, w_out,
          norm_ffn_g, w_up, ffn_conv_w, w_down, final_norm_g):
    for l in range(DEPTH):
        x = x + mixer(x, norm_mix_g[l], w_in[l], sc_conv_w[l], cf_conv_w[l], cf_conv_b[l],
                      cf_ln_g[l], cf_ln_b[l], w_out[l])
        x = x + conv_ffn(x, norm_ffn_g[l], w_up[l], ffn_conv_w[l], w_down[l])
    return rmsnorm(x, final_norm_g)


def setup_inputs(seed: int = 0) -> dict:
    key = jax.random.key(seed)
    ks = jax.random.split(key, 18)
    f32 = jnp.float32
    nrm = lambda k, shape, scale: jax.random.normal(k, shape, f32) * scale
    return {
        "x_prompt": nrm(ks[0], (BATCH, SEQ, D_MODEL), 1.0),
        "x_sample": nrm(ks[1], (DEC_BATCH, DEC_SEQ, D_MODEL), 1.0),
        "norm_mix_g": 1.0 + nrm(ks[2], (DEPTH, D_MODEL), 0.02),
        "w_in": nrm(ks[3], (DEPTH, D_MODEL, D_IN), D_MODEL ** -0.5),
        "sc_conv_w": nrm(ks[4], (DEPTH, SC_KERNEL, D_SC), SC_KERNEL ** -0.5),
        "cf_conv_w": nrm(ks[5], (DEPTH, CF_KERNEL, D_CF), CF_KERNEL ** -0.5),
        "cf_conv_b": nrm(ks[6], (DEPTH, D_CF), 0.02),
        "cf_ln_g": 1.0 + nrm(ks[7], (DEPTH, D_CF), 0.02),
        "cf_ln_b": nrm(ks[8], (DEPTH, D_CF), 0.02),
        "w_out": nrm(ks[9], (DEPTH, D_MIX, D_MODEL), D_MIX ** -0.5),
        "norm_ffn_g": 1.0 + nrm(ks[10], (DEPTH, D_MODEL), 0.02),
        "w_up": nrm(ks[11], (DEPTH, D_MODEL, 2 * D_FF), D_MODEL ** -0.5),
        "ffn_conv_w": nrm(ks[12], (DEPTH, FFN_KERNEL, D_FF), FFN_KERNEL ** -0.5),
        "w_down": nrm(ks[13], (DEPTH, D_FF, D_MODEL), D_FF ** -0.5),
        "final_norm_g": 1.0 + nrm(ks[14], (D_MODEL,), 0.02),
    }


def reference(x_prompt, x_sample, norm_mix_g, w_in, sc_conv_w, cf_conv_w, cf_conv_b, cf_ln_g,
              cf_ln_b, w_out, norm_ffn_g, w_up, ffn_conv_w, w_down, final_norm_g):
    y_prompt = trunk(x_prompt, norm_mix_g, w_in, sc_conv_w, cf_conv_w, cf_conv_b, cf_ln_g,
                     cf_ln_b, w_out, norm_ffn_g, w_up, ffn_conv_w, w_down, final_norm_g)
    y_sample = trunk(x_sample, norm_mix_g, w_in, sc_conv_w, cf_conv_w, cf_conv_b, cf_ln_g,
                     cf_ln_b, w_out, norm_ffn_g, w_up, ffn_conv_w, w_down, final_norm_g)
    return (y_prompt, y_sample)
```

```python
import functools
from typing import NamedTuple

import numpy as np
import jax
import jax.numpy as jnp
from jax import lax
from jax.experimental import pallas as pl
from jax.experimental.pallas import tpu as pltpu

F32 = jnp.float32
BF16 = jnp.bfloat16
EPS = 1e-6
HEAD_DIM = 128
SC_TAPS = 3
CF_TAPS = 31
HALO = 16
F32_SUBLANES = 8
VMEM_LIMIT_BYTES = 56 * 1024 * 1024


class Cfg(NamedTuple):
    d_model: int
    d_sc: int
    d_f: int
    d_cf: int
    d_ff: int
    rows_p: int
    seq_p: int
    rows_s: int
    seq_s: int
    tm: int
    t_norm: int
    tn: int
    tm_o: int
    tn_o: int
    tm_d: int
    tn_d: int
    tm_f: int
    tn_f: int

    @property
    def rows(self):
        return self.rows_p + self.rows_s

    @property
    def m_ext(self):
        return self.tm + 2 * HALO


def _cparams(n_axes):
    return pltpu.CompilerParams(
        dimension_semantics=("arbitrary",) * n_axes,
        vmem_limit_bytes=VMEM_LIMIT_BYTES)


def _dot(a, b):
    return jnp.dot(a, b, preferred_element_type=F32)


def _rms(x, g):
    r = lax.rsqrt(jnp.mean(x * x, axis=-1, keepdims=True) + EPS)
    return (x * r) * g


def _norm_ext_kernel(x_ref, xp_ref, xn_ref, g_ref, o_ref, *, cfg):
    i = pl.program_id(0)
    s = pl.program_id(1)
    g = g_ref[...]

    @pl.when(s == 0)
    def _():
        r0 = i * cfg.tm
        r1 = r0 + cfg.tm
        in_p = r0 < cfg.rows_p
        at_start = jnp.where(in_p, (r0 & (cfg.seq_p - 1)) == 0, (r0 & (cfg.seq_s - 1)) == 0)
        at_end = jnp.where(in_p, (r1 & (cfg.seq_p - 1)) == 0, (r1 & (cfg.seq_s - 1)) == 0)
        keep_prev = jnp.where(at_start, 0.0, 1.0).astype(F32)
        keep_next = jnp.where(at_end, 0.0, 1.0).astype(F32)
        o_ref[0:HALO, :] = (_rms(xp_ref[...], g) * keep_prev).astype(BF16)
        o_ref[HALO + cfg.tm:2 * HALO + cfg.tm, :] = (_rms(xn_ref[...], g) * keep_next).astype(BF16)

    row = pl.multiple_of(HALO + s * cfg.t_norm, HALO)
    o_ref[pl.ds(row, cfg.t_norm), :] = _rms(x_ref[...], g).astype(BF16)


def _norm_ext(x, g, cfg):
    n_tiles = cfg.rows // cfg.tm
    n_sub = cfg.tm // cfg.t_norm
    halo_blocks = cfg.rows // HALO
    per_tile = cfg.tm // HALO
    d = cfg.d_model
    return pl.pallas_call(
        functools.partial(_norm_ext_kernel, cfg=cfg),
        grid=(n_tiles, n_sub),
        in_specs=[
            pl.BlockSpec((cfg.t_norm, d), lambda i, s: (i * n_sub + s, 0)),
            pl.BlockSpec((HALO, d), lambda i, s: (jnp.maximum(i * per_tile - 1, 0), 0)),
            pl.BlockSpec((HALO, d), lambda i, s: (jnp.minimum((i + 1) * per_tile, halo_blocks - 1), 0)),
            pl.BlockSpec((1, d), lambda i, s: (0, 0)),
        ],
        out_specs=pl.BlockSpec((None, cfg.m_ext, d), lambda i, s: (i, 0, 0)),
        out_shape=jax.ShapeDtypeStruct((n_tiles, cfg.m_ext, d), BF16),
        compiler_params=_cparams(2),
        name="norm_ext",
    )(x, x, x, g.reshape(1, d))


def _final_norm_kernel(x_ref, g_ref, o_ref):
    o_ref[...] = _rms(x_ref[...], g_ref[...])


def _final_norm(x, g, row0, n_rows, cfg):
    d = cfg.d_model
    t = cfg.t_norm
    off = row0 // t
    return pl.pallas_call(
        _final_norm_kernel,
        grid=(n_rows // t,),
        in_specs=[pl.BlockSpec((t, d), lambda i: (i + off, 0)),
                  pl.BlockSpec((1, d), lambda i: (0, 0))],
        out_specs=pl.BlockSpec((t, d), lambda i: (i, 0)),
        out_shape=jax.ShapeDtypeStruct((n_rows, d), F32),
        compiler_params=_cparams(1),
        name="final_norm",
    )(x, g.reshape(1, d))


def _conv3_rows(z, w, tm):
    m = z.shape[0]
    z_prev = pltpu.roll(z, 1, 0)[HALO:HALO + tm]
    z_next = pltpu.roll(z, m - 1, 0)[HALO:HALO + tm]
    return w[0:1] * z_prev + w[1:2] * z[HALO:HALO + tm] + w[2:3] * z_next


def _sc_kernel(h_ref, wb_ref, wc_ref, wv_ref, cw_ref, o_ref, *, tm):
    h = h_ref[...]
    z = _dot(h, wc_ref[...]) * _dot(h, wv_ref[...])
    conv = _conv3_rows(z, cw_ref[...], tm)
    b = _dot(h_ref[HALO:HALO + tm, :], wb_ref[...])
    o_ref[...] = (b * conv).astype(BF16)


def _sc(h_ext, w_in, conv_w, cfg):
    n_tiles = cfg.rows // cfg.tm
    tn = cfg.tn
    nj = cfg.d_sc // tn
    d = cfg.d_model
    return pl.pallas_call(
        functools.partial(_sc_kernel, tm=cfg.tm),
        grid=(n_tiles, nj),
        in_specs=[
            pl.BlockSpec((None, cfg.m_ext, d), lambda i, j: (i, 0, 0)),
            pl.BlockSpec((d, tn), lambda i, j: (0, j)),
            pl.BlockSpec((d, tn), lambda i, j: (0, j + nj)),
            pl.BlockSpec((d, tn), lambda i, j: (0, j + 2 * nj)),
            pl.BlockSpec((SC_TAPS, tn), lambda i, j: (0, j)),
        ],
        out_specs=pl.BlockSpec((cfg.tm, tn), lambda i, j: (i, j)),
        out_shape=jax.ShapeDtypeStruct((cfg.rows, cfg.d_sc), BF16),
        compiler_params=_cparams(2),
        name="mix_sc",
    )(h_ext, w_in, w_in, w_in, conv_w)


def _fproj_kernel(h_ref, w_ref, cs_ref, zr_ref, zi_ref, *, tm, tn):
    f = _dot(h_ref[HALO:HALO + tm, :], w_ref[...]).astype(BF16)
    for g in range(tn // HEAD_DIM):
        lo, hi = g * HEAD_DIM, (g + 1) * HEAD_DIM
        z = _dot(f[:, lo:hi], cs_ref[...])
        zr_ref[:, lo:hi] = z[:, :HEAD_DIM].astype(BF16)
        zi_ref[:, lo:hi] = z[:, HEAD_DIM:].astype(BF16)


def _fproj(h_ext, w_in, cfg):
    n_tiles = cfg.rows // cfg.tm
    tn = cfg.tn
    d = cfg.d_model
    col0 = 3 * cfg.d_sc // tn
    k = np.arange(HEAD_DIM)
    ang = 2.0 * np.pi * ((k[:, None] * k[None, :]) % HEAD_DIM) / HEAD_DIM
    cs = jnp.asarray(np.concatenate([np.cos(ang), -np.sin(ang)], axis=1), BF16)
    out = jax.ShapeDtypeStruct((cfg.rows, cfg.d_f), BF16)
    return pl.pallas_call(
        functools.partial(_fproj_kernel, tm=cfg.tm, tn=tn),
        grid=(n_tiles, cfg.d_f // tn),
        in_specs=[
            pl.BlockSpec((None, cfg.m_ext, d), lambda i, j: (i, 0, 0)),
            pl.BlockSpec((d, tn), lambda i, j: (0, j + col0)),
            pl.BlockSpec((HEAD_DIM, 2 * HEAD_DIM), lambda i, j: (0, 0)),
        ],
        out_specs=[pl.BlockSpec((cfg.tm, tn), lambda i, j: (i, j))] * 2,
        out_shape=[out, out],
        compiler_params=_cparams(2),
        name="mix_fproj",
    )(h_ext, w_in, cs)


def _dft_tables(seq):
    k = jnp.arange(seq, dtype=jnp.int32)
    ang = ((k[:, None] * k[None, :]) & (seq - 1)).astype(F32) * (2.0 * np.pi / seq)
    scale = 1.0 / np.sqrt(seq * HEAD_DIM)
    return (jnp.cos(ang) * scale).astype(BF16), (jnp.sin(ang) * scale).astype(BF16)


def _dft_kernel(cos_ref, sin_ref, zr_ref, zi_ref, *rest):
    o_ref = rest[-1]
    o_ref[...] = (_dot(cos_ref[...], zr_ref[...]) + _dot(sin_ref[...], zi_ref[...])).astype(BF16)


def _seq_dft(zr, zi, tables, row0, n_rows, seq, prev_out, cfg):
    cos_t, sin_t = tables
    tm, tn = min(cfg.tm_f, seq), cfg.tn_f
    nb = n_rows // seq
    b0 = row0 // seq
    ni = seq // tm
    in_specs = [
        pl.BlockSpec((tm, seq), lambda b, n, i: (i, 0)),
        pl.BlockSpec((tm, seq), lambda b, n, i: (i, 0)),
        pl.BlockSpec((seq, tn), lambda b, n, i: (b + b0, n)),
        pl.BlockSpec((seq, tn), lambda b, n, i: (b + b0, n)),
    ]
    args = [cos_t, sin_t, zr, zi]
    aliases = {}
    if prev_out is not None:
        in_specs.append(pl.BlockSpec(memory_space=pl.ANY))
        args.append(prev_out)
        aliases = {4: 0}
    return pl.pallas_call(
        _dft_kernel,
        grid=(nb, cfg.d_f // tn, ni),
        in_specs=in_specs,
        out_specs=pl.BlockSpec((tm, tn), lambda b, n, i: ((b + b0) * ni + i, n)),
        out_shape=jax.ShapeDtypeStruct((cfg.rows, cfg.d_f), BF16),
        input_output_aliases=aliases,
        compiler_params=_cparams(3),
        name="mix_dft",
    )(*args)


def _cf_kernel(h_ref, wu_ref, wg_ref, cw_ref, cb_ref, lg_ref, lb_ref, o_ref, gc_ref, *, tm, tn, nj):
    j = pl.program_id(1)
    h = h_ref[...]
    g = _dot(h, wu_ref[...]) * jax.nn.sigmoid(_dot(h, wg_ref[...]))
    m = g.shape[0]
    cw = cw_ref[...]
    acc = jnp.zeros((tm, tn), F32)
    for r in range(F32_SUBLANES):
        g_r = g if r == 0 else pltpu.roll(g, m - r, 0)
        for q in range((CF_TAPS + F32_SUBLANES) // F32_SUBLANES):
            k = F32_SUBLANES * q + r - 1
            if 0 <= k < CF_TAPS:
                acc = acc + cw[k:k + 1] * g_r[F32_SUBLANES * q:F32_SUBLANES * q + tm]
    gc_ref[j] = acc + cb_ref[...]

    @pl.when(j == nj - 1)
    def _():
        d_cf = nj * tn
        total = jnp.zeros((tm, 1), F32)
        for jj in range(nj):
            total = total + jnp.sum(gc_ref[jj], axis=1, keepdims=True)
        mu = total * (1.0 / d_cf)
        sq = jnp.zeros((tm, 1), F32)
        for jj in range(nj):
            c = gc_ref[jj] - mu
            sq = sq + jnp.sum(c * c, axis=1, keepdims=True)
        rstd = lax.rsqrt(sq * (1.0 / d_cf) + EPS)
        for jj in range(nj):
            lo, hi = jj * tn, (jj + 1) * tn
            y = ((gc_ref[jj] - mu) * rstd) * lg_ref[:, lo:hi] + lb_ref[:, lo:hi]
            o_ref[:, lo:hi] = (y * jax.nn.sigmoid(y)).astype(BF16)


def _cf(h_ext, w_in, conv_w, conv_b, ln_g, ln_b, cfg):
    n_tiles = cfg.rows // cfg.tm
    tn = cfg.tn
    nj = cfg.d_cf // tn
    d = cfg.d_model
    col_u = (3 * cfg.d_sc + cfg.d_f) // tn
    col_g = col_u + nj
    return pl.pallas_call(
        functools.partial(_cf_kernel, tm=cfg.tm, tn=tn, nj=nj),
        grid=(n_tiles, nj),
        in_specs=[
            pl.BlockSpec((None, cfg.m_ext, d), lambda i, j: (i, 0, 0)),
            pl.BlockSpec((d, tn), lambda i, j: (0, j + col_u)),
            pl.BlockSpec((d, tn), lambda i, j: (0, j + col_g)),
            pl.BlockSpec((CF_TAPS, tn), lambda i, j: (0, j)),
            pl.BlockSpec((1, tn), lambda i, j: (0, j)),
            pl.BlockSpec((1, cfg.d_cf), lambda i, j: (0, 0)),
            pl.BlockSpec((1, cfg.d_cf), lambda i, j: (0, 0)),
        ],
        out_specs=pl.BlockSpec((cfg.tm, cfg.d_cf), lambda i, j: (i, 0)),
        out_shape=jax.ShapeDtypeStruct((cfg.rows, cfg.d_cf), BF16),
        scratch_shapes=[pltpu.VMEM((nj, cfg.tm, tn), F32)],
        compiler_params=_cparams(2),
        name="mix_cf",
    )(h_ext, w_in, w_in, conv_w, conv_b.reshape(1, -1), ln_g.reshape(1, -1), ln_b.reshape(1, -1))


def _outproj_kernel(ya_ref, yb_ref, yc_ref, wa_ref, wb_ref, wc_ref, x_ref, o_ref):
    acc = _dot(ya_ref[...], wa_ref[...]) + _dot(yb_ref[...], wb_ref[...]) + _dot(yc_ref[...], wc_ref[...])
    o_ref[...] = x_ref[...] + acc


def _outproj(y_sc, y_f, y_cf, w_out, x, cfg):
    tm, tn = cfg.tm_o, cfg.tn_o
    d = cfg.d_model
    wa = w_out[:cfg.d_sc]
    wb = w_out[cfg.d_sc:cfg.d_sc + cfg.d_f]
    wc = w_out[cfg.d_sc + cfg.d_f:]
    return pl.pallas_call(
        _outproj_kernel,
        grid=(cfg.rows // tm, d // tn),
        in_specs=[
            pl.BlockSpec((tm, cfg.d_sc), lambda i, j: (i, 0)),
            pl.BlockSpec((tm, cfg.d_f), lambda i, j: (i, 0)),
            pl.BlockSpec((tm, cfg.d_cf), lambda i, j: (i, 0)),
            pl.BlockSpec((cfg.d_sc, tn), lambda i, j: (0, j)),
            pl.BlockSpec((cfg.d_f, tn), lambda i, j: (0, j)),
            pl.BlockSpec((cfg.d_cf, tn), lambda i, j: (0, j)),
            pl.BlockSpec((tm, tn), lambda i, j: (i, j)),
        ],
        out_specs=pl.BlockSpec((tm, tn), lambda i, j: (i, j)),
        out_shape=jax.ShapeDtypeStruct((cfg.rows, d), F32),
        compiler_params=_cparams(2),
        name="mix_outproj",
    )(y_sc, y_f, y_cf, wa, wb, wc, x)


def _up_kernel(h_ref, wg_ref, wv_ref, cw_ref, o_ref, *, tm):
    gate = _conv3_rows(_dot(h_ref[...], wg_ref[...]), cw_ref[...], tm)
    val = _dot(h_ref[HALO:HALO + tm, :], wv_ref[...])
    o_ref[...] = ((gate * jax.nn.sigmoid(gate)) * val).astype(BF16)


def _up(h_ext, w_up, conv_w, cfg):
    n_tiles = cfg.rows // cfg.tm
    tn = cfg.tn
    nj = cfg.d_ff // tn
    d = cfg.d_model
    return pl.pallas_call(
        functools.partial(_up_kernel, tm=cfg.tm),
        grid=(n_tiles, nj),
        in_specs=[
            pl.BlockSpec((None, cfg.m_ext, d), lambda i, j: (i, 0, 0)),
            pl.BlockSpec((d, tn), lambda i, j: (0, j)),
            pl.BlockSpec((d, tn), lambda i, j: (0, j + nj)),
            pl.BlockSpec((SC_TAPS, tn), lambda i, j: (0, j)),
        ],
        out_specs=pl.BlockSpec((cfg.tm, tn), lambda i, j: (i, j)),
        out_shape=jax.ShapeDtypeStruct((cfg.rows, cfg.d_ff), BF16),
        compiler_params=_cparams(2),
        name="ffn_up",
    )(h_ext, w_up, w_up, conv_w)


def _down_kernel(u_ref, w_ref, x_ref, o_ref):
    o_ref[...] = x_ref[...] + _dot(u_ref[...], w_ref[...])


def _down(u, w_down, x, cfg):
    tm, tn = cfg.tm_d, cfg.tn_d
    d = cfg.d_model
    return pl.pallas_call(
        _down_kernel,
        grid=(cfg.rows // tm, d // tn),
        in_specs=[
            pl.BlockSpec((tm, cfg.d_ff), lambda i, j: (i, 0)),
            pl.BlockSpec((cfg.d_ff, tn), lambda i, j: (0, j)),
            pl.BlockSpec((tm, tn), lambda i, j: (i, j)),
        ],
        out_specs=pl.BlockSpec((tm, tn), lambda i, j: (i, j)),
        out_shape=jax.ShapeDtypeStruct((cfg.rows, d), F32),
        compiler_params=_cparams(2),
        name="ffn_down",
    )(u, w_down, x)


def _trunk(x, p, cfg):
    (norm_mix_g, w_in, sc_conv_w, cf_conv_w, cf_conv_b, cf_ln_g, cf_ln_b, w_out,
     norm_ffn_g, w_up, ffn_conv_w, w_down, final_norm_g) = p
    tables_p = _dft_tables(cfg.seq_p)
    tables_s = _dft_tables(cfg.seq_s)
    for l in range(w_in.shape[0]):
        w_in_l = w_in[l].astype(BF16)
        w_out_l = w_out[l].astype(BF16)
        w_up_l = w_up[l].astype(BF16)
        w_down_l = w_down[l].astype(BF16)

        h = _norm_ext(x, norm_mix_g[l], cfg)
        y_sc = _sc(h, w_in_l, sc_conv_w[l], cfg)
        zr, zi = _fproj(h, w_in_l, cfg)
        y_f = _seq_dft(zr, zi, tables_p, 0, cfg.rows_p, cfg.seq_p, None, cfg)
        y_f = _seq_dft(zr, zi, tables_s, cfg.rows_p, cfg.rows_s, cfg.seq_s, y_f, cfg)
        y_cf = _cf(h, w_in_l, cf_conv_w[l], cf_conv_b[l], cf_ln_g[l], cf_ln_b[l], cfg)
        x = _outproj(y_sc, y_f, y_cf, w_out_l, x, cfg)

        h = _norm_ext(x, norm_ffn_g[l], cfg)
        u = _up(h, w_up_l, ffn_conv_w[l], cfg)
        x = _down(u, w_down_l, x, cfg)
    y_p = _final_norm(x, final_norm_g, 0, cfg.rows_p, cfg)
    y_s = _final_norm(x, final_norm_g, cfg.rows_p, cfg.rows_s, cfg)
    return y_p, y_s


def _make_cfg(x_prompt, x_sample, w_up):
    bp, sp, d = x_prompt.shape
    bs, ss, _ = x_sample.shape
    d_ff = w_up.shape[-1] // 2
    n_heads = d // HEAD_DIM
    n_f = n_heads // 4
    n_sc = (n_heads - n_f) // 2
    n_cf = n_heads - n_f - n_sc
    return Cfg(d_model=d, d_sc=n_sc * HEAD_DIM, d_f=n_f * HEAD_DIM, d_cf=n_cf * HEAD_DIM, d_ff=d_ff,
               rows_p=bp * sp, seq_p=sp, rows_s=bs * ss, seq_s=ss,
               tm=1024, t_norm=256, tn=256, tm_o=1024, tn_o=512, tm_d=512, tn_d=512,
               tm_f=512, tn_f=512)


def kernel(x_prompt, x_sample, norm_mix_g, w_in, sc_conv_w, cf_conv_w, cf_conv_b, cf_ln_g, cf_ln_b, w_out, norm_ffn_g, w_up, ffn_conv_w, w_down, final_norm_g):
    cfg = _make_cfg(x_prompt, x_sample, w_up)
    x = jnp.concatenate([x_prompt.reshape(cfg.rows_p, cfg.d_model),
                         x_sample.reshape(cfg.rows_s, cfg.d_model)], axis=0)
    params = (norm_mix_g, w_in, sc_conv_w, cf_conv_w, cf_conv_b, cf_ln_g, cf_ln_b, w_out,
              norm_ffn_g, w_up, ffn_conv_w, w_down, final_norm_g)
    y_p, y_s = _trunk(x, params, cfg)
    return y_p.reshape(x_prompt.shape), y_s.reshape(x_sample.shape)
```

```python
import functools
from typing import NamedTuple

import numpy as np
import jax
import jax.numpy as jnp
from jax import lax
from jax.experimental import pallas as pl
from jax.experimental.pallas import tpu as pltpu

F32 = jnp.float32
BF16 = jnp.bfloat16
EPS = 1e-6
HEAD_DIM = 128
SC_TAPS = 3
CF_TAPS = 31
HALO = 16
F32_SUBLANES = 8
VMEM_LIMIT_BYTES = 56 * 1024 * 1024


class Cfg(NamedTuple):
    d_model: int
    d_sc: int
    d_f: int
    d_cf: int
    d_ff: int
    rows_p: int
    seq_p: int
    rows_s: int
    seq_s: int
    tm: int
    tm_ffn: int
    t_norm: int
    tn: int
    tm_o: int
    tn_o: int
    tm_d: int
    tn_d: int
    tm_f: int
    tn_f: int

    @property
    def rows(self):
        return self.rows_p + self.rows_s

    @property
    def m_ext(self):
        return self.tm + 2 * HALO


def _cparams(n_axes):
    return pltpu.CompilerParams(
        dimension_semantics=("arbitrary",) * n_axes,
        vmem_limit_bytes=VMEM_LIMIT_BYTES)


def _dot(a, b):
    return jnp.dot(a, b, preferred_element_type=F32)


def _rms(x, g):
    r = lax.rsqrt(jnp.mean(x * x, axis=-1, keepdims=True) + EPS)
    return (x * r) * g


class Part(NamedTuple):
    x: jax.Array
    src_row0: int
    row0: int
    n_rows: int
    seq: int


def _norm_ext_kernel(x_ref, xp_ref, xn_ref, g_ref, *rest, tm, t_norm, seq):
    o_ref = rest[-1]
    i = pl.program_id(0)
    s = pl.program_id(1)
    g = g_ref[...]

    @pl.when(s == 0)
    def _():
        r0 = i * tm
        keep_prev = jnp.where((r0 & (seq - 1)) == 0, 0.0, 1.0).astype(F32)
        keep_next = jnp.where(((r0 + tm) & (seq - 1)) == 0, 0.0, 1.0).astype(F32)
        o_ref[0:HALO, :] = (_rms(xp_ref[...], g) * keep_prev).astype(BF16)
        o_ref[HALO + tm:2 * HALO + tm, :] = (_rms(xn_ref[...], g) * keep_next).astype(BF16)

    row = pl.multiple_of(HALO + s * t_norm, HALO)
    o_ref[pl.ds(row, t_norm), :] = _rms(x_ref[...], g).astype(BF16)


def _norm_ext_part(part, g, tm, prev_out, cfg):
    t_norm = cfg.t_norm
    n_sub = tm // t_norm
    per_tile = tm // HALO
    d = cfg.d_model
    sub0 = part.src_row0 // t_norm
    halo0 = part.src_row0 // HALO
    halo_last = part.x.shape[0] // HALO - 1
    tile0 = part.row0 // tm
    in_specs = [
        pl.BlockSpec((t_norm, d), lambda i, s: (sub0 + i * n_sub + s, 0)),
        pl.BlockSpec((HALO, d), lambda i, s: (jnp.maximum(halo0 + i * per_tile - 1, 0), 0)),
        pl.BlockSpec((HALO, d), lambda i, s: (jnp.minimum(halo0 + (i + 1) * per_tile, halo_last), 0)),
        pl.BlockSpec((1, d), lambda i, s: (0, 0)),
    ]
    args = [part.x, part.x, part.x, g.reshape(1, d)]
    aliases = {}
    if prev_out is not None:
        in_specs.append(pl.BlockSpec(memory_space=pl.ANY))
        args.append(prev_out)
        aliases = {4: 0}
    return pl.pallas_call(
        functools.partial(_norm_ext_kernel, tm=tm, t_norm=t_norm, seq=part.seq),
        grid=(part.n_rows // tm, n_sub),
        in_specs=in_specs,
        out_specs=pl.BlockSpec((None, tm + 2 * HALO, d), lambda i, s: (tile0 + i, 0, 0)),
        out_shape=jax.ShapeDtypeStruct((cfg.rows // tm, tm + 2 * HALO, d), BF16),
        input_output_aliases=aliases,
        compiler_params=_cparams(2),
        name="norm_ext",
    )(*args)


def _norm_ext(parts, g, tm, cfg):
    out = None
    for part in parts:
        out = _norm_ext_part(part, g, tm, out, cfg)
    return out


def _final_norm_kernel(x_ref, g_ref, o_ref):
    o_ref[...] = _rms(x_ref[...], g_ref[...])


def _final_norm(x, g, row0, n_rows, cfg):
    d = cfg.d_model
    t = cfg.t_norm
    off = row0 // t
    return pl.pallas_call(
        _final_norm_kernel,
        grid=(n_rows // t,),
        in_specs=[pl.BlockSpec((t, d), lambda i: (i + off, 0)),
                  pl.BlockSpec((1, d), lambda i: (0, 0))],
        out_specs=pl.BlockSpec((t, d), lambda i: (i, 0)),
        out_shape=jax.ShapeDtypeStruct((n_rows, d), F32),
        compiler_params=_cparams(1),
        name="final_norm",
    )(x, g.reshape(1, d))


def _conv3_rows(z, w, tm):
    m = z.shape[0]
    z_prev = pltpu.roll(z, 1, 0)[HALO:HALO + tm]
    z_next = pltpu.roll(z, m - 1, 0)[HALO:HALO + tm]
    return w[0:1] * z_prev + w[1:2] * z[HALO:HALO + tm] + w[2:3] * z_next


def _sc_kernel(h_ref, wb_ref, wc_ref, wv_ref, cw_ref, o_ref, *, tm):
    h = h_ref[...]
    z = _dot(h, wc_ref[...]) * _dot(h, wv_ref[...])
    conv = _conv3_rows(z, cw_ref[...], tm)
    b = _dot(h_ref[HALO:HALO + tm, :], wb_ref[...])
    o_ref[...] = (b * conv).astype(BF16)


def _sc(h_ext, w_in, layer, conv_w, cfg):
    n_tiles = cfg.rows // cfg.tm
    tn = cfg.tn
    nj = cfg.d_sc // tn
    d = cfg.d_model
    return pl.pallas_call(
        functools.partial(_sc_kernel, tm=cfg.tm),
        grid=(n_tiles, nj),
        in_specs=[
            pl.BlockSpec((None, cfg.m_ext, d), lambda i, j: (i, 0, 0)),
            pl.BlockSpec((None, d, tn), lambda i, j: (layer, 0, j)),
            pl.BlockSpec((None, d, tn), lambda i, j: (layer, 0, j + nj)),
            pl.BlockSpec((None, d, tn), lambda i, j: (layer, 0, j + 2 * nj)),
            pl.BlockSpec((SC_TAPS, tn), lambda i, j: (0, j)),
        ],
        out_specs=pl.BlockSpec((cfg.tm, tn), lambda i, j: (i, j)),
        out_shape=jax.ShapeDtypeStruct((cfg.rows, cfg.d_sc), BF16),
        compiler_params=_cparams(2),
        name="mix_sc",
    )(h_ext, w_in, w_in, w_in, conv_w)


def _fproj_kernel(h_ref, w_ref, cs_ref, zr_ref, zi_ref, *, tm, tn):
    f = _dot(h_ref[HALO:HALO + tm, :], w_ref[...]).astype(BF16)
    for g in range(tn // HEAD_DIM):
        lo, hi = g * HEAD_DIM, (g + 1) * HEAD_DIM
        z = _dot(f[:, lo:hi], cs_ref[...])
        zr_ref[:, lo:hi] = z[:, :HEAD_DIM].astype(BF16)
        zi_ref[:, lo:hi] = z[:, HEAD_DIM:].astype(BF16)


def _fproj(h_ext, w_in, layer, cfg):
    n_tiles = cfg.rows // cfg.tm
    tn = cfg.tn
    d = cfg.d_model
    col0 = 3 * cfg.d_sc // tn
    k = np.arange(HEAD_DIM)
    ang = 2.0 * np.pi * ((k[:, None] * k[None, :]) % HEAD_DIM) / HEAD_DIM
    cs = jnp.asarray(np.concatenate([np.cos(ang), -np.sin(ang)], axis=1), BF16)
    out = jax.ShapeDtypeStruct((cfg.rows, cfg.d_f), BF16)
    return pl.pallas_call(
        functools.partial(_fproj_kernel, tm=cfg.tm, tn=tn),
        grid=(n_tiles, cfg.d_f // tn),
        in_specs=[
            pl.BlockSpec((None, cfg.m_ext, d), lambda i, j: (i, 0, 0)),
            pl.BlockSpec((None, d, tn), lambda i, j: (layer, 0, j + col0)),
            pl.BlockSpec((HEAD_DIM, 2 * HEAD_DIM), lambda i, j: (0, 0)),
        ],
        out_specs=[pl.BlockSpec((cfg.tm, tn), lambda i, j: (i, j))] * 2,
        out_shape=[out, out],
        compiler_params=_cparams(2),
        name="mix_fproj",
    )(h_ext, w_in, cs)


def _dft_tables(seq):
    m = 64
    k = jnp.arange(seq, dtype=jnp.int32)[:, None]
    w = 2.0 * np.pi / seq
    a = ((k * (m * jnp.arange(seq // m, dtype=jnp.int32))[None, :]) & (seq - 1)).astype(F32) * w
    b = ((k * jnp.arange(m, dtype=jnp.int32)[None, :]) & (seq - 1)).astype(F32) * w
    scale = 1.0 / np.sqrt(seq * HEAD_DIM)
    ca, sa = (jnp.cos(a) * scale)[:, :, None], (jnp.sin(a) * scale)[:, :, None]
    cb, sb = jnp.cos(b)[:, None, :], jnp.sin(b)[:, None, :]
    cos_t = (ca * cb - sa * sb).reshape(seq, seq).astype(BF16)
    sin_t = (sa * cb + ca * sb).reshape(seq, seq).astype(BF16)
    return cos_t, sin_t


def _dft_kernel(cos_ref, sin_ref, zr_ref, zi_ref, *rest):
    o_ref = rest[-1]
    o_ref[...] = (_dot(cos_ref[...], zr_ref[...]) + _dot(sin_ref[...], zi_ref[...])).astype(BF16)


def _seq_dft(zr, zi, tables, row0, n_rows, seq, prev_out, cfg):
    cos_t, sin_t = tables
    tm, tn = min(cfg.tm_f, seq), cfg.tn_f
    nb = n_rows // seq
    b0 = row0 // seq
    ni = seq // tm
    in_specs = [
        pl.BlockSpec((tm, seq), lambda b, n, i: (i, 0)),
        pl.BlockSpec((tm, seq), lambda b, n, i: (i, 0)),
        pl.BlockSpec((seq, tn), lambda b, n, i: (b + b0, n)),
        pl.BlockSpec((seq, tn), lambda b, n, i: (b + b0, n)),
    ]
    args = [cos_t, sin_t, zr, zi]
    aliases = {}
    if prev_out is not None:
        in_specs.append(pl.BlockSpec(memory_space=pl.ANY))
        args.append(prev_out)
        aliases = {4: 0}
    return pl.pallas_call(
        _dft_kernel,
        grid=(nb, cfg.d_f // tn, ni),
        in_specs=in_specs,
        out_specs=pl.BlockSpec((tm, tn), lambda b, n, i: ((b + b0) * ni + i, n)),
        out_shape=jax.ShapeDtypeStruct((cfg.rows, cfg.d_f), BF16),
        input_output_aliases=aliases,
        compiler_params=_cparams(3),
        name="mix_dft",
    )(*args)


def _cf_kernel(h_ref, wu_ref, wg_ref, cw_ref, cb_ref, lg_ref, lb_ref, o_ref,
               gc_ref, piv_ref, s1_ref, s2_ref, *, tm, tn, nj):
    j = pl.program_id(1)
    h = h_ref[...]
    g = _dot(h, wu_ref[...]) * jax.nn.sigmoid(_dot(h, wg_ref[...]))
    m = g.shape[0]
    cw = cw_ref[...]
    acc = jnp.zeros((tm, tn), F32)
    for r in range(F32_SUBLANES):
        g_r = g if r == 0 else pltpu.roll(g, m - r, 0)
        for q in range((CF_TAPS + F32_SUBLANES) // F32_SUBLANES):
            k = F32_SUBLANES * q + r - 1
            if 0 <= k < CF_TAPS:
                acc = acc + cw[k:k + 1] * g_r[F32_SUBLANES * q:F32_SUBLANES * q + tm]
    acc = acc + cb_ref[...]
    gc_ref[j] = acc

    @pl.when(j == 0)
    def _():
        piv_ref[...] = jnp.mean(acc, axis=1, keepdims=True)
        s1_ref[...] = jnp.zeros(s1_ref.shape, F32)
        s2_ref[...] = jnp.zeros(s2_ref.shape, F32)

    c = acc - piv_ref[...]
    s1_ref[...] += jnp.sum(c, axis=1, keepdims=True)
    s2_ref[...] += jnp.sum(c * c, axis=1, keepdims=True)

    @pl.when(j == nj - 1)
    def _():
        inv_d = 1.0 / (nj * tn)
        m1 = s1_ref[...] * inv_d
        mu = piv_ref[...] + m1
        rstd = lax.rsqrt(s2_ref[...] * inv_d - m1 * m1 + EPS)
        for jj in range(nj):
            lo, hi = jj * tn, (jj + 1) * tn
            y = ((gc_ref[jj] - mu) * rstd) * lg_ref[:, lo:hi] + lb_ref[:, lo:hi]
            o_ref[:, lo:hi] = (y * jax.nn.sigmoid(y)).astype(BF16)


def _cf(h_ext, w_in, layer, conv_w, conv_b, ln_g, ln_b, cfg):
    n_tiles = cfg.rows // cfg.tm
    tn = cfg.tn
    nj = cfg.d_cf // tn
    d = cfg.d_model
    col_u = (3 * cfg.d_sc + cfg.d_f) // tn
    col_g = col_u + nj
    stat = pltpu.VMEM((cfg.tm, 1), F32)
    return pl.pallas_call(
        functools.partial(_cf_kernel, tm=cfg.tm, tn=tn, nj=nj),
        grid=(n_tiles, nj),
        in_specs=[
            pl.BlockSpec((None, cfg.m_ext, d), lambda i, j: (i, 0, 0)),
            pl.BlockSpec((None, d, tn), lambda i, j: (layer, 0, j + col_u)),
            pl.BlockSpec((None, d, tn), lambda i, j: (layer, 0, j + col_g)),
            pl.BlockSpec((CF_TAPS, tn), lambda i, j: (0, j)),
            pl.BlockSpec((1, tn), lambda i, j: (0, j)),
            pl.BlockSpec((1, cfg.d_cf), lambda i, j: (0, 0)),
            pl.BlockSpec((1, cfg.d_cf), lambda i, j: (0, 0)),
        ],
        out_specs=pl.BlockSpec((cfg.tm, cfg.d_cf), lambda i, j: (i, 0)),
        out_shape=jax.ShapeDtypeStruct((cfg.rows, cfg.d_cf), BF16),
        scratch_shapes=[pltpu.VMEM((nj, cfg.tm, tn), F32), stat, stat, stat],
        compiler_params=_cparams(2),
        name="mix_cf",
    )(h_ext, w_in, w_in, conv_w, conv_b.reshape(1, -1), ln_g.reshape(1, -1), ln_b.reshape(1, -1))


def _outproj_kernel(ya_ref, yb_ref, yc_ref, w_ref, x_ref, *rest, d_sc, d_f):
    o_ref = rest[-1]
    acc = (_dot(ya_ref[...], w_ref[0:d_sc, :])
           + _dot(yb_ref[...], w_ref[d_sc:d_sc + d_f, :])
           + _dot(yc_ref[...], w_ref[d_sc + d_f:, :]))
    o_ref[...] = x_ref[...] + acc


def _outproj_part(y_sc, y_f, y_cf, w_out, layer, part, prev_out, cfg):
    tm, tn = cfg.tm_o, cfg.tn_o
    d = cfg.d_model
    t_src = part.src_row0 // tm
    t_dst = part.row0 // tm
    in_specs = [
        pl.BlockSpec((tm, cfg.d_sc), lambda i, j: (t_dst + i, 0)),
        pl.BlockSpec((tm, cfg.d_f), lambda i, j: (t_dst + i, 0)),
        pl.BlockSpec((tm, cfg.d_cf), lambda i, j: (t_dst + i, 0)),
        pl.BlockSpec((None, d, tn), lambda i, j: (layer, 0, j)),
        pl.BlockSpec((tm, tn), lambda i, j: (t_src + i, j)),
    ]
    args = [y_sc, y_f, y_cf, w_out, part.x]
    aliases = {}
    if prev_out is not None:
        in_specs.append(pl.BlockSpec(memory_space=pl.ANY))
        args.append(prev_out)
        aliases = {5: 0}
    return pl.pallas_call(
        functools.partial(_outproj_kernel, d_sc=cfg.d_sc, d_f=cfg.d_f),
        grid=(part.n_rows // tm, d // tn),
        in_specs=in_specs,
        out_specs=pl.BlockSpec((tm, tn), lambda i, j: (t_dst + i, j)),
        out_shape=jax.ShapeDtypeStruct((cfg.rows, d), F32),
        input_output_aliases=aliases,
        compiler_params=_cparams(2),
        name="mix_outproj",
    )(*args)


def _outproj(y_sc, y_f, y_cf, w_out, layer, parts, cfg):
    out = None
    for part in parts:
        out = _outproj_part(y_sc, y_f, y_cf, w_out, layer, part, out, cfg)
    return out


def _up_kernel(h_ref, wg_ref, wv_ref, cw_ref, o_ref, *, tm):
    gate = _conv3_rows(_dot(h_ref[...], wg_ref[...]), cw_ref[...], tm)
    val = _dot(h_ref[HALO:HALO + tm, :], wv_ref[...])
    o_ref[...] = ((gate * jax.nn.sigmoid(gate)) * val).astype(BF16)


def _up(h_ext, w_up, layer, conv_w, cfg):
    tm = cfg.tm_ffn
    n_tiles = cfg.rows // tm
    tn = cfg.tn
    nj = cfg.d_ff // tn
    d = cfg.d_model
    return pl.pallas_call(
        functools.partial(_up_kernel, tm=tm),
        grid=(n_tiles, nj),
        in_specs=[
            pl.BlockSpec((None, tm + 2 * HALO, d), lambda i, j: (i, 0, 0)),
            pl.BlockSpec((None, d, tn), lambda i, j: (layer, 0, j)),
            pl.BlockSpec((None, d, tn), lambda i, j: (layer, 0, j + nj)),
            pl.BlockSpec((SC_TAPS, tn), lambda i, j: (0, j)),
        ],
        out_specs=pl.BlockSpec((tm, tn), lambda i, j: (i, j)),
        out_shape=jax.ShapeDtypeStruct((cfg.rows, cfg.d_ff), BF16),
        compiler_params=_cparams(2),
        name="ffn_up",
    )(h_ext, w_up, w_up, conv_w)


def _down_kernel(u_ref, w_ref, x_ref, o_ref):
    o_ref[...] = x_ref[...] + _dot(u_ref[...], w_ref[...])


def _down(u, w_down, layer, x, cfg):
    tm, tn = cfg.tm_d, cfg.tn_d
    d = cfg.d_model
    return pl.pallas_call(
        _down_kernel,
        grid=(cfg.rows // tm, d // tn),
        in_specs=[
            pl.BlockSpec((tm, cfg.d_ff), lambda i, j: (i, 0)),
            pl.BlockSpec((None, cfg.d_ff, tn), lambda i, j: (layer, 0, j)),
            pl.BlockSpec((tm, tn), lambda i, j: (i, j)),
        ],
        out_specs=pl.BlockSpec((tm, tn), lambda i, j: (i, j)),
        out_shape=jax.ShapeDtypeStruct((cfg.rows, d), F32),
        compiler_params=_cparams(2),
        name="ffn_down",
    )(u, w_down, x)


def _trunk(x_p, x_s, p, cfg):
    (norm_mix_g, w_in, sc_conv_w, cf_conv_w, cf_conv_b, cf_ln_g, cf_ln_b, w_out,
     norm_ffn_g, w_up, ffn_conv_w, w_down, final_norm_g) = p
    tables_p = _dft_tables(cfg.seq_p)
    tables_s = _dft_tables(cfg.seq_s)
    w_in, w_out, w_up, w_down = (w.astype(BF16) for w in (w_in, w_out, w_up, w_down))

    def flat_parts(x):
        return [Part(x, 0, 0, cfg.rows_p, cfg.seq_p),
                Part(x, cfg.rows_p, cfg.rows_p, cfg.rows_s, cfg.seq_s)]

    parts = [Part(x_p, 0, 0, cfg.rows_p, cfg.seq_p), Part(x_s, 0, cfg.rows_p, cfg.rows_s, cfg.seq_s)]
    for l in range(w_in.shape[0]):
        h = _norm_ext(parts, norm_mix_g[l], cfg.tm, cfg)
        y_sc = _sc(h, w_in, l, sc_conv_w[l], cfg)
        zr, zi = _fproj(h, w_in, l, cfg)
        y_f = _seq_dft(zr, zi, tables_p, 0, cfg.rows_p, cfg.seq_p, None, cfg)
        y_f = _seq_dft(zr, zi, tables_s, cfg.rows_p, cfg.rows_s, cfg.seq_s, y_f, cfg)
        y_cf = _cf(h, w_in, l, cf_conv_w[l], cf_conv_b[l], cf_ln_g[l], cf_ln_b[l], cfg)
        x = _outproj(y_sc, y_f, y_cf, w_out, l, parts, cfg)

        h = _norm_ext(flat_parts(x), norm_ffn_g[l], cfg.tm_ffn, cfg)
        u = _up(h, w_up, l, ffn_conv_w[l], cfg)
        x = _down(u, w_down, l, x, cfg)
        parts = flat_parts(x)
    y_p = _final_norm(x, final_norm_g, 0, cfg.rows_p, cfg)
    y_s = _final_norm(x, final_norm_g, cfg.rows_p, cfg.rows_s, cfg)
    return y_p, y_s


def _make_cfg(x_prompt, x_sample, w_up):
    bp, sp, d = x_prompt.shape
    bs, ss, _ = x_sample.shape
    d_ff = w_up.shape[-1] // 2
    n_heads = d // HEAD_DIM
    n_f = n_heads // 4
    n_sc = (n_heads - n_f) // 2
    n_cf = n_heads - n_f - n_sc
    return Cfg(d_model=d, d_sc=n_sc * HEAD_DIM, d_f=n_f * HEAD_DIM, d_cf=n_cf * HEAD_DIM, d_ff=d_ff,
               rows_p=bp * sp, seq_p=sp, rows_s=bs * ss, seq_s=ss,
               tm=1024, tm_ffn=1024, t_norm=256, tn=256, tm_o=1024, tn_o=512, tm_d=512, tn_d=512,
               tm_f=512, tn_f=512)


def kernel(x_prompt, x_sample, norm_mix_g, w_in, sc_conv_w, cf_conv_w, cf_conv_b, cf_ln_g, cf_ln_b, w_out, norm_ffn_g, w_up, ffn_conv_w, w_down, final_norm_g):
    cfg = _make_cfg(x_prompt, x_sample, w_up)
    params = (norm_mix_g, w_in, sc_conv_w, cf_conv_w, cf_conv_b, cf_ln_g, cf_ln_b, w_out,
              norm_ffn_g, w_up, ffn_conv_w, w_down, final_norm_g)
    y_p, y_s = _trunk(x_prompt.reshape(cfg.rows_p, cfg.d_model),
                      x_sample.reshape(cfg.rows_s, cfg.d_model), params, cfg)
    return y_p.reshape(x_prompt.shape), y_s.reshape(x_sample.shape)
```
